```python
import jax, jax.numpy as jnp
from jax import lax
import numpy as np

D_MODEL = 2048
BATCH = 4
SEQ = 4096
DEPTH = 1

PLE_DIM = 256
EPS = 1e-6
NEG = -1e30
FORCE = 1e3
MAX_POS_OFFSET = 1024

M_WIDTH = D_MODEL // 2
M_HEADS = 4
M_HEAD_DIM = M_WIDTH // M_HEADS
M_CONV = 4
M_CHUNK = 64

N_WIDTH = D_MODEL - M_WIDTH
N_HEAD_DIM = 64
N_HEADS = N_WIDTH // N_HEAD_DIM
N_KV = 4
CMP_BLOCK = 32
CMP_STRIDE = 16
CMP_HIDDEN = 128
SEL_BLOCK = 64
N_SELECT = 16
WINDOW = 512
NSA_Q_BLOCK = 64
ROPE_THETA = 500000.0
ROT_DIM = N_HEAD_DIM // 4

D_MIX = M_WIDTH + N_WIDTH
KV_W = N_KV * N_HEAD_DIM
IN_SPLITS = (M_WIDTH,) * 5 + (M_HEADS, M_HEADS) + (N_WIDTH,) + (KV_W,) * 6 + (3 * N_HEADS, N_WIDTH)
D_IN = sum(IN_SPLITS)

kernel_name = "hybrid_mlstm_nsa_parallel_heads"


def rmsnorm(x, g):
    xf = x.astype(jnp.float32)
    y = xf * lax.rsqrt(jnp.mean(xf * xf, axis=-1, keepdims=True) + EPS)
    return (y * g.astype(jnp.float32)).astype(x.dtype)


def split_cols(h, sizes):
    cuts = [int(c) for c in np.cumsum(sizes)[:-1]]
    return jnp.split(h, cuts, axis=-1)


def heads(a, n):
    return a.reshape(a.shape[0], a.shape[1], n, -1)


def partial_rope(x, positions):
    half = ROT_DIM // 2
    inv = ROPE_THETA ** (-(jnp.arange(half, dtype=jnp.float32) * 2.0 / ROT_DIM))
    ang = positions.astype(jnp.float32)[..., None] * inv
    cos = jnp.cos(ang)[:, :, None, :]
    sin = jnp.sin(ang)[:, :, None, :]
    xr = x[..., :ROT_DIM].astype(jnp.float32)
    x1, x2 = xr[..., :half], xr[..., half:]
    rot = jnp.concatenate([x1 * cos - x2 * sin, x2 * cos + x1 * sin], axis=-1).astype(x.dtype)
    return jnp.concatenate([rot, x[..., ROT_DIM:]], axis=-1)


def causal_conv(a, w, b):
    S = a.shape[1]
    ap = jnp.pad(a, ((0, 0), (M_CONV - 1, 0), (0, 0)))
    y = b + w[M_CONV - 1] * a
    for j in range(M_CONV - 1):
        y = y + w[j] * ap[:, j:j + S]
    return y


def head_layernorm(h, g):
    mu = jnp.mean(h, axis=-1, keepdims=True)
    var = jnp.mean(jnp.square(h - mu), axis=-1, keepdims=True)
    y = (h - mu) * lax.rsqrt(var + EPS)
    return y.reshape(h.shape[0], h.shape[1], -1) * g.astype(jnp.float32)


def mlstm_chunk_step(carry, xs):
    C, n, m = carry
    q, k, v, ig, lf = xs
    L = q.shape[2]
    b = jnp.cumsum(lf, axis=-1)
    causal = jnp.tril(jnp.ones((L, L), dtype=bool))
    D = jnp.where(causal, b[..., :, None] - b[..., None, :] + ig[..., None, :], NEG)
    m_inter = b + m[..., None]
    m_t = jnp.maximum(m_inter, jnp.max(D, axis=-1))
    A = jnp.exp(D - m_t[..., None]) * jnp.einsum('bhtd,bhsd->bhts', q, k)
    inter = jnp.exp(m_inter - m_t)
    num = jnp.einsum('bhts,bhsd->bhtd', A, v) + inter[..., None] * jnp.einsum('bhtk,bhvk->bhtv', q, C)
    den = jnp.sum(A, axis=-1) + inter * jnp.einsum('bhtk,bhk->bht', q, n)
    h = num / jnp.maximum(jnp.abs(den), 1.0)[..., None]
    bL = b[..., -1]
    w_end = bL[..., None] - b + ig
    m_new = jnp.maximum(bL + m, jnp.max(w_end, axis=-1))
    w_s = jnp.exp(w_end - m_new[..., None])
    decay = jnp.exp(bL + m - m_new)
    C = decay[..., None, None] * C + jnp.einsum('bhs,bhsv,bhsk->bhvk', w_s, v, k)
    n = decay[..., None] * n + jnp.einsum('bhs,bhsk->bhk', w_s, k)
    return (C, n, m_new), h


def mlstm_mixer(q, k, v, ig, lf):
    B, S, H, dh = q.shape
    L = M_CHUNK
    NC = S // L

    def chunks(a):
        return a.reshape(B, NC, L, H, dh).transpose(1, 0, 3, 2, 4)

    def gchunks(a):
        return a.reshape(B, NC, L, H).transpose(1, 0, 3, 2)

    init = (jnp.zeros((B, H, dh, dh), jnp.float32), jnp.zeros((B, H, dh), jnp.float32),
            jnp.zeros((B, H), jnp.float32))
    _, hs = lax.scan(mlstm_chunk_step, init, (chunks(q), chunks(k), chunks(v), gchunks(ig), gchunks(lf)))
    return hs.transpose(1, 0, 3, 2, 4).reshape(B, S, H, dh)


def compress(a, pos, w1, w2):
    B, S, G, dh = a.shape
    R = CMP_BLOCK // CMP_STRIDE
    n16 = S // CMP_STRIDE
    nc = n16 - R + 1
    a16 = a.reshape(B, n16, CMP_STRIDE, G, dh)
    blocks = jnp.concatenate([a16[:, r:r + nc] for r in range(R)], axis=2)
    blocks = blocks + pos[None, None, :, None, :]
    flat = blocks.transpose(0, 1, 3, 2, 4).reshape(B, nc, G, CMP_BLOCK * dh)
    return jax.nn.gelu(flat @ w1) @ w2


def sel_overlap(nc, nsel):
    tok = np.arange(nc)[:, None] * CMP_STRIDE + np.arange(CMP_BLOCK)[None, :]
    onehot = (tok[..., None] // SEL_BLOCK) == np.arange(nsel)
    return jnp.asarray(onehot.mean(axis=1).astype(np.float32))


def nsa_mixer(q, kc, vc, ks, vs, kw, vw, gates, pos_k, pos_v, ck_w1, ck_w2, cv_w1, cv_w2):
    B, S, H, dh = q.shape
    G = kc.shape[2]
    J = H // G
    QB = NSA_Q_BLOCK
    NB = S // QB
    kcmp = compress(kc, pos_k, ck_w1, ck_w2)
    vcmp = compress(vc, pos_v, cv_w1, cv_w2)
    nc = kcmp.shape[1]
    c_end = jnp.arange(nc) * CMP_STRIDE + CMP_BLOCK - 1
    nsel = S // SEL_BLOCK
    n_top = min(N_SELECT, nsel)
    overlap = sel_overlap(nc, nsel)
    ksb = ks.reshape(B, nsel, SEL_BLOCK, G, dh).transpose(0, 3, 1, 2, 4)
    vsb = vs.reshape(B, nsel, SEL_BLOCK, G, dh).transpose(0, 3, 1, 2, 4)
    kw_pad = jnp.pad(kw, ((0, 0), (WINDOW, 0), (0, 0), (0, 0)))
    vw_pad = jnp.pad(vw, ((0, 0), (WINDOW, 0), (0, 0), (0, 0)))
    bi = jnp.arange(B)[:, None, None, None]
    gi = jnp.arange(G)[None, :, None, None]
    blk = jnp.arange(nsel)

    def block_fn(args):
        c, qc, gc = args
        t = c * QB + jnp.arange(QB)
        qg = qc.reshape(B, QB, G, J, dh).transpose(0, 2, 3, 1, 4)
        mask_c = c_end[None, :] <= t[:, None]
        s = jnp.einsum('bgjqd,bcgd->bgjqc', qg, kcmp).astype(jnp.float32)
        p_c = jax.nn.softmax(jnp.where(mask_c, s, NEG), axis=-1) * mask_c
        o_cmp = jnp.einsum('bgjqc,bcgd->bgjqd', p_c.astype(vcmp.dtype), vcmp)
        imp = jnp.einsum('bgjqc,cn->bgqn', p_c, overlap)
        cur = (t // SEL_BLOCK)[:, None]
        valid = blk[None, :] * SEL_BLOCK <= t[:, None]
        forced = (blk[None, :] == 0) | (blk[None, :] == cur) | (blk[None, :] == cur - 1)
        score = jnp.where(valid, imp + jnp.where(forced, FORCE, 0.0), NEG)
        _, idx = lax.top_k(score, n_top)
        kg = ksb[bi, gi, idx]
        vg = vsb[bi, gi, idx]
        kpos = idx[..., None] * SEL_BLOCK + jnp.arange(SEL_BLOCK)
        mask_s = (kpos <= t[:, None, None])[:, :, None]
        s = jnp.einsum('bgjqd,bgqnld->bgjqnl', qg, kg).astype(jnp.float32)
        s = jnp.where(mask_s, s, NEG).reshape(B, G, J, QB, n_top * SEL_BLOCK)
        p_s = jax.nn.softmax(s, axis=-1).reshape(B, G, J, QB, n_top, SEL_BLOCK)
        o_sel = jnp.einsum('bgjqnl,bgqnld->bgjqd', p_s.astype(vg.dtype), vg)
        kwc = lax.dynamic_slice_in_dim(kw_pad, c * QB, QB + WINDOW, axis=1)
        vwc = lax.dynamic_slice_in_dim(vw_pad, c * QB, QB + WINDOW, axis=1)
        kp = c * QB - WINDOW + jnp.arange(QB + WINDOW)
        mask_w = (kp[None, :] <= t[:, None]) & (kp[None, :] > t[:, None] - WINDOW) & (kp[None, :] >= 0)
        s = jnp.einsum('bgjqd,bkgd->bgjqk', qg, kwc).astype(jnp.float32)
        p_w = jax.nn.softmax(jnp.where(mask_w, s, NEG), axis=-1)
        o_win = jnp.einsum('bgjqk,bkgd->bgjqd', p_w.astype(vwc.dtype), vwc)
        g = gc.reshape(B, QB, G, J, 3).transpose(0, 2, 3, 1, 4)
        o = g[..., 0:1] * o_cmp + g[..., 1:2] * o_sel + g[..., 2:3] * o_win
        return o.transpose(0, 3, 1, 2, 4).reshape(B, QB, H * dh)

    q_blocks = q.reshape(B, NB, QB, H, dh).swapaxes(0, 1)
    g_blocks = gates.reshape(B, NB, QB, H, 3).swapaxes(0, 1)
    out = lax.map(block_fn, (jnp.arange(NB), q_blocks, g_blocks))
    return out.swapaxes(0, 1).reshape(B, S, H * dh)


def setup_inputs(seed: int = 0) -> dict:
    key = jax.random.key(seed)
    k = jax.random.split(key, 24)
    f32 = jnp.float32

    def nrm(kk, shape, scale):
        return jax.random.normal(kk, shape, f32) * scale

    x = nrm(k[0], (BATCH, SEQ, D_MODEL), 1.0)
    p = nrm(k[1], (DEPTH, BATCH, SEQ, PLE_DIM), 1.0)
    offset = jax.random.randint(k[2], (BATCH, 1), 0, MAX_POS_OFFSET, dtype=jnp.int32)
    positions = offset + jnp.arange(SEQ, dtype=jnp.int32)[None, :]
    norm_g = 1.0 + nrm(k[3], (DEPTH, D_MODEL), 0.02)
    w_in = nrm(k[4], (DEPTH, D_MODEL, D_IN), D_MODEL ** -0.5)
    conv_w = nrm(k[5], (DEPTH, M_CONV, 2 * M_WIDTH), M_CONV ** -0.5)
    conv_b = nrm(k[6], (DEPTH, 2 * M_WIDTH), 0.01)
    b_igate = nrm(k[7], (DEPTH, M_HEADS), 0.1)
    b_fgate = jnp.linspace(3.0, 6.0, M_HEADS, dtype=f32)[None, :] + nrm(k[8], (DEPTH, M_HEADS), 0.1)
    m_norm_g = 1.0 + nrm(k[9], (DEPTH, M_WIDTH), 0.02)
    cmp_pos_k = nrm(k[10], (DEPTH, CMP_BLOCK, N_HEAD_DIM), 0.02)
    cmp_pos_v = nrm(k[11], (DEPTH, CMP_BLOCK, N_HEAD_DIM), 0.02)
    cmp_k_w1 = nrm(k[12], (DEPTH, CMP_BLOCK * N_HEAD_DIM, CMP_HIDDEN), (CMP_BLOCK * N_HEAD_DIM) ** -0.5)
    cmp_k_w2 = nrm(k[13], (DEPTH, CMP_HIDDEN, N_HEAD_DIM), CMP_HIDDEN ** -0.5)
    cmp_v_w1 = nrm(k[14], (DEPTH, CMP_BLOCK * N_HEAD_DIM, CMP_HIDDEN), (CMP_BLOCK * N_HEAD_DIM) ** -0.5)
    cmp_v_w2 = nrm(k[15], (DEPTH, CMP_HIDDEN, N_HEAD_DIM), CMP_HIDDEN ** -0.5)
    w_out = nrm(k[16], (DEPTH, D_MIX, D_MODEL), D_MIX ** -0.5)
    ple_gate_w = nrm(k[17], (DEPTH, D_MODEL, D_MODEL), D_MODEL ** -0.5)
    ple_proj_w = nrm(k[18], (DEPTH, PLE_DIM, D_MODEL), PLE_DIM ** -0.5)
    ple_norm_g = 1.0 + nrm(k[19], (DEPTH, D_MODEL), 0.02)
    final_norm_g = 1.0 + nrm(k[20], (D_MODEL,), 0.02)
    return {"x": x, "p": p, "positions": positions, "norm_g": norm_g, "w_in": w_in,
            "conv_w": conv_w, "conv_b": conv_b, "b_igate": b_igate, "b_fgate": b_fgate,
            "m_norm_g": m_norm_g, "cmp_pos_k": cmp_pos_k, "cmp_pos_v": cmp_pos_v,
            "cmp_k_w1": cmp_k_w1, "cmp_k_w2": cmp_k_w2, "cmp_v_w1": cmp_v_w1, "cmp_v_w2": cmp_v_w2,
            "w_out": w_out, "ple_gate_w": ple_gate_w, "ple_proj_w": ple_proj_w,
            "ple_norm_g": ple_norm_g, "final_norm_g": final_norm_g}


def reference(x, p, positions, norm_g, w_in, conv_w, conv_b, b_igate, b_fgate, m_norm_g,
              cmp_pos_k, cmp_pos_v, cmp_k_w1, cmp_k_w2, cmp_v_w1, cmp_v_w2, w_out,
              ple_gate_w, ple_proj_w, ple_norm_g, final_norm_g):
    B, S, _ = x.shape
    f32 = jnp.float32
    for l in range(DEPTH):
        h = rmsnorm(x, norm_g[l])
        (mq, mk, mv, mo, mz, mi, mf, nq, kc, vc, ks, vs, kw, vw, ng, nz) = split_cols(h @ w_in[l], IN_SPLITS)
        qk = jax.nn.silu(causal_conv(jnp.concatenate([mq, mk], axis=-1), conv_w[l], conv_b[l]))
        mq_c, mk_c = qk[..., :M_WIDTH], qk[..., M_WIDTH:]
        ig = mi.astype(f32) + b_igate[l].astype(f32)
        lf = jax.nn.log_sigmoid(mf.astype(f32) + b_fgate[l].astype(f32))
        hm = mlstm_mixer(heads(mq_c, M_HEADS).astype(f32),
                         (heads(mk_c, M_HEADS) * (M_HEAD_DIM ** -0.5)).astype(f32),
                         heads(mv, M_HEADS).astype(f32), ig, lf)
        hm = head_layernorm(hm, m_norm_g[l]).astype(x.dtype)
        m_out = hm * jax.nn.sigmoid(mo) * jax.nn.silu(mz)
        q = partial_rope(heads(nq, N_HEADS) * (N_HEAD_DIM ** -0.5), positions)
        n_att = nsa_mixer(q,
                          partial_rope(heads(kc, N_KV), positions), heads(vc, N_KV),
                          partial_rope(heads(ks, N_KV), positions), heads(vs, N_KV),
                          partial_rope(heads(kw, N_KV), positions), heads(vw, N_KV),
                          jax.nn.sigmoid(ng).reshape(B, S, N_HEADS, 3),
                          cmp_pos_k[l], cmp_pos_v[l], cmp_k_w1[l], cmp_k_w2[l], cmp_v_w1[l], cmp_v_w2[l])
        n_out = n_att * jax.nn.silu(nz)
        x = x + jnp.concatenate([m_out, n_out], axis=-1) @ w_out[l]
        ple = rmsnorm(p[l] @ ple_proj_w[l], ple_norm_g[l])
        x = x + jax.nn.sigmoid(x @ ple_gate_w[l]) * ple
    return rmsnorm(x, final_norm_g)
```

```python
import functools

import jax
import jax.numpy as jnp
import numpy as np
from jax import lax
from jax.experimental import pallas as pl
from jax.experimental.pallas import tpu as pltpu

F32 = jnp.float32
BF16 = jnp.bfloat16

D_MODEL = 2048
PLE_DIM = 256
EPS = 1e-6
NEG = -1e30
FORCE = 1e3

M_WIDTH = D_MODEL // 2
M_HEADS = 4
M_HEAD_DIM = M_WIDTH // M_HEADS
M_CONV = 4

N_WIDTH = D_MODEL - M_WIDTH
N_HEAD_DIM = 64
N_HEADS = N_WIDTH // N_HEAD_DIM
N_KV = 4
N_REP = N_HEADS // N_KV
CMP_BLOCK = 32
CMP_STRIDE = 16
CMP_HIDDEN = 128
SEL_BLOCK = 64
N_SELECT = 16
WINDOW = 512
ROPE_THETA = 500000.0
ROT_DIM = N_HEAD_DIM // 4
KV_W = N_KV * N_HEAD_DIM

LANES = 128
VMEM_LIMIT = 56 * 1024 * 1024

PROJ_TM = 512
PROJ_F32_TN = 512
PROJ_BF16_TN = 256
MLSTM_L = 256
CONV_PAD = 8
NSA_TQ = 256
NSA_TK = 256
OUT_TM = 256

F32_COLS = 5 * M_WIDTH
BF16_COLS = M_WIDTH + N_WIDTH + 6 * KV_W
ROPE_LO = M_WIDTH // PROJ_BF16_TN
ROPE_HI = (M_WIDTH + N_WIDTH + 3 * KV_W) // PROJ_BF16_TN
GATE_NG0 = 2 * M_HEADS


def _nt_dot(a, b):
    return lax.dot_general(a, b, (((1,), (1,)), ((), ())), preferred_element_type=F32)


def _div_sel(t):
    return jnp.right_shift(t, SEL_BLOCK.bit_length() - 1)


def _sigmoid(x):
    return 1.0 / (1.0 + jnp.exp(-x))


def _rms_rows(x, g):
    var = jnp.mean(x * x, axis=-1, keepdims=True)
    return (x * lax.rsqrt(var + EPS)) * g


def _proj_f32_kernel(x_ref, g_ref, w_ref, ws_ref, o_ref, os_ref, h_ref):
    @pl.when(pl.program_id(1) == 0)
    def _():
        h = _rms_rows(x_ref[...], g_ref[...]).astype(BF16)
        h_ref[...] = h
        os_ref[...] = jnp.dot(h, ws_ref[...], preferred_element_type=F32)

    o_ref[...] = jnp.dot(h_ref[...], w_ref[...], preferred_element_type=F32)


def _proj_bf16_kernel(x_ref, g_ref, pos_ref, inv_ref, sa_ref, sb_ref, w_ref, o_ref,
                      h_ref, cos_ref, sina_ref, sinb_ref):
    j = pl.program_id(1)

    @pl.when(j == 0)
    def _():
        h_ref[...] = _rms_rows(x_ref[...], g_ref[...]).astype(BF16)
        ang = pos_ref[...] * inv_ref[...]
        sin = jnp.sin(ang)
        cos_ref[...] = jnp.cos(ang)
        sina_ref[...] = sin * sa_ref[...]
        sinb_ref[...] = sin * sb_ref[...]

    y = jnp.dot(h_ref[...], w_ref[...], preferred_element_type=F32)
    is_rope = jnp.logical_and(j >= ROPE_LO, j < ROPE_HI)

    @pl.when(is_rope)
    def _():
        reps = PROJ_BF16_TN // LANES
        c = jnp.concatenate([cos_ref[...]] * reps, axis=1)
        sa = jnp.concatenate([sina_ref[...]] * reps, axis=1)
        sb = jnp.concatenate([sinb_ref[...]] * reps, axis=1)
        half = ROT_DIM // 2
        up = pltpu.roll(y, PROJ_BF16_TN - half, axis=1)
        dn = pltpu.roll(y, half, axis=1)
        o_ref[...] = (y * c + up * sa + dn * sb).astype(BF16)

    @pl.when(jnp.logical_not(is_rope))
    def _():
        o_ref[...] = y.astype(BF16)


def _input_projections(x2, norm_g, pos_col, w_f32, w_small, w_bf16, rope_consts):
    T = x2.shape[0]
    tm = PROJ_TM
    g2 = norm_g.reshape(1, D_MODEL)
    yf, ys = pl.pallas_call(
        _proj_f32_kernel,
        grid=(T // tm, F32_COLS // PROJ_F32_TN),
        in_specs=[
            pl.BlockSpec((tm, D_MODEL), lambda i, j: (i, 0)),
            pl.BlockSpec((1, D_MODEL), lambda i, j: (0, 0)),
            pl.BlockSpec((D_MODEL, PROJ_F32_TN), lambda i, j: (0, j)),
            pl.BlockSpec((D_MODEL, LANES), lambda i, j: (0, 0)),
        ],
        out_specs=[
            pl.BlockSpec((tm, PROJ_F32_TN), lambda i, j: (i, j)),
            pl.BlockSpec((tm, LANES), lambda i, j: (i, 0)),
        ],
        out_shape=[jax.ShapeDtypeStruct((T, F32_COLS), F32),
                   jax.ShapeDtypeStruct((T, LANES), F32)],
        scratch_shapes=[pltpu.VMEM((tm, D_MODEL), BF16)],
        compiler_params=pltpu.CompilerParams(
            dimension_semantics=("arbitrary", "arbitrary"), vmem_limit_bytes=VMEM_LIMIT),
        name="proj_f32",
    )(x2, g2, w_f32, w_small)

    inv_lane, sa_lane, sb_lane = rope_consts
    yb = pl.pallas_call(
        _proj_bf16_kernel,
        grid=(T // tm, BF16_COLS // PROJ_BF16_TN),
        in_specs=[
            pl.BlockSpec((tm, D_MODEL), lambda i, j: (i, 0)),
            pl.BlockSpec((1, D_MODEL), lambda i, j: (0, 0)),
            pl.BlockSpec((tm, 1), lambda i, j: (i, 0)),
            pl.BlockSpec((1, LANES), lambda i, j: (0, 0)),
            pl.BlockSpec((1, LANES), lambda i, j: (0, 0)),
            pl.BlockSpec((1, LANES), lambda i, j: (0, 0)),
            pl.BlockSpec((D_MODEL, PROJ_BF16_TN), lambda i, j: (0, j)),
        ],
        out_specs=pl.BlockSpec((tm, PROJ_BF16_TN), lambda i, j: (i, j)),
        out_shape=jax.ShapeDtypeStruct((T, BF16_COLS), BF16),
        scratch_shapes=[pltpu.VMEM((tm, D_MODEL), BF16),
                        pltpu.VMEM((tm, LANES), F32),
                        pltpu.VMEM((tm, LANES), F32),
                        pltpu.VMEM((tm, LANES), F32)],
        compiler_params=pltpu.CompilerParams(
            dimension_semantics=("arbitrary", "arbitrary"), vmem_limit_bytes=VMEM_LIMIT),
        name="proj_bf16",
    )(x2, g2, pos_col, inv_lane, sa_lane, sb_lane, w_bf16)
    return yf, ys, yb


def _mlstm_kernel(qk_ref, oz_ref, v_ref, gs_ref, cw_ref, cb_ref, bi_ref, bf_ref, ng_ref, o_ref,
                  cbuf, ct_ref, n_ref, m_ref):
    L = MLSTM_L
    dh = M_HEAD_DIM

    @pl.when(pl.program_id(1) == 0)
    def _():
        cbuf[0:CONV_PAD, :] = jnp.zeros((CONV_PAD, 2 * M_WIDTH), F32)
        ct_ref[...] = jnp.zeros_like(ct_ref)
        n_ref[...] = jnp.zeros_like(n_ref)
        m_ref[...] = jnp.zeros_like(m_ref)

    cbuf[CONV_PAD:CONV_PAD + L, :] = qk_ref[...]
    y = cb_ref[...] + cw_ref[M_CONV - 1:M_CONV, :] * cbuf[CONV_PAD:CONV_PAD + L, :]
    for j in range(M_CONV - 1):
        off = CONV_PAD - (M_CONV - 1) + j
        y = y + cw_ref[j:j + 1, :] * cbuf[off:off + L, :]
    qk = y * _sigmoid(y)
    cbuf[0:CONV_PAD, :] = cbuf[L:L + CONV_PAD, :]

    gs = gs_ref[...]
    ig_all = gs + bi_ref[...]
    z = gs + bf_ref[...]
    lf_all = jnp.minimum(z, 0.0) - jnp.log1p(jnp.exp(-jnp.abs(z)))
    row = lax.broadcasted_iota(jnp.int32, (L, L), 0)
    col = lax.broadcasted_iota(jnp.int32, (L, L), 1)
    causal = row >= col
    tri = jnp.where(causal, 1.0, 0.0).astype(F32)
    b_all = jnp.dot(tri, lf_all, precision=lax.Precision.HIGHEST,
                    preferred_element_type=F32)
    ig_t = ig_all.T
    b_t = b_all.T

    for h in range(M_HEADS):
        q = qk[:, h * dh:(h + 1) * dh]
        k = qk[:, M_WIDTH + h * dh:M_WIDTH + (h + 1) * dh] * (M_HEAD_DIM ** -0.5)
        v = v_ref[:, h * dh:(h + 1) * dh]
        qb = q.astype(BF16)
        b_col = b_all[:, M_HEADS + h:M_HEADS + h + 1]
        ig_col = ig_all[:, h:h + 1]
        b_row = b_t[M_HEADS + h:M_HEADS + h + 1, :]
        ig_row = ig_t[h:h + 1, :]
        m_prev = m_ref[h]
        ct = ct_ref[h]
        n_prev = n_ref[h]

        d = jnp.where(causal, b_col - b_row + ig_row, NEG)
        m_inter = b_col + m_prev
        m_t = jnp.maximum(m_inter, jnp.max(d, axis=-1, keepdims=True))
        a = jnp.exp(d - m_t) * _nt_dot(qb, k.astype(BF16))
        inter = jnp.exp(m_inter - m_t)
        num = (jnp.dot(a.astype(BF16), v, preferred_element_type=F32)
               + inter * jnp.dot(qb, ct.astype(BF16), preferred_element_type=F32))
        den = (jnp.sum(a, axis=-1, keepdims=True)
               + inter * jnp.sum(q * n_prev, axis=-1, keepdims=True))
        hh = num / jnp.maximum(jnp.abs(den), 1.0)

        b_last = b_all[L - 1:L, M_HEADS + h:M_HEADS + h + 1]
        m_new = jnp.maximum(b_last + m_prev,
                            jnp.max(b_last - b_row + ig_row, axis=-1, keepdims=True))
        w_col = jnp.exp(b_last - b_col + ig_col - m_new)
        decay = jnp.exp(b_last + m_prev - m_new)
        kw = k * w_col
        ct_ref[h] = decay * ct + lax.dot_general(
            kw.astype(BF16), v, (((0,), (0,)), ((), ())), preferred_element_type=F32)
        n_ref[h] = decay * n_prev + jnp.sum(kw, axis=0, keepdims=True)
        m_ref[h] = m_new

        mu = jnp.mean(hh, axis=-1, keepdims=True)
        cen = hh - mu
        var = jnp.mean(cen * cen, axis=-1, keepdims=True)
        yn = cen * lax.rsqrt(var + EPS) * ng_ref[:, h * dh:(h + 1) * dh]
        mo = oz_ref[:, h * dh:(h + 1) * dh]
        mz = oz_ref[:, M_WIDTH + h * dh:M_WIDTH + (h + 1) * dh]
        o_ref[:, h * dh:(h + 1) * dh] = (yn * _sigmoid(mo) * (mz * _sigmoid(mz))).astype(BF16)


def _mlstm(yf, yb, ys, conv_w, conv_b, bi_lane, bf_lane, m_norm_g, B, S):
    L = MLSTM_L
    nc = S // L
    T = B * S
    return pl.pallas_call(
        _mlstm_kernel,
        grid=(B, nc),
        in_specs=[
            pl.BlockSpec((L, 2 * M_WIDTH), lambda b, c: (b * nc + c, 0)),
            pl.BlockSpec((L, 2 * M_WIDTH), lambda b, c: (b * nc + c, 1)),
            pl.BlockSpec((L, M_WIDTH), lambda b, c: (b * nc + c, 0)),
            pl.BlockSpec((L, LANES), lambda b, c: (b * nc + c, 0)),
            pl.BlockSpec((M_CONV, 2 * M_WIDTH), lambda b, c: (0, 0)),
            pl.BlockSpec((1, 2 * M_WIDTH), lambda b, c: (0, 0)),
            pl.BlockSpec((1, LANES), lambda b, c: (0, 0)),
            pl.BlockSpec((1, LANES), lambda b, c: (0, 0)),
            pl.BlockSpec((1, M_WIDTH), lambda b, c: (0, 0)),
        ],
        out_specs=pl.BlockSpec((L, M_WIDTH), lambda b, c: (b * nc + c, 0)),
        out_shape=jax.ShapeDtypeStruct((T, M_WIDTH), BF16),
        scratch_shapes=[pltpu.VMEM((L + CONV_PAD, 2 * M_WIDTH), F32),
                        pltpu.VMEM((M_HEADS, M_HEAD_DIM, M_HEAD_DIM), F32),
                        pltpu.VMEM((M_HEADS, 1, M_HEAD_DIM), F32),
                        pltpu.VMEM((M_HEADS, 1, 1), F32)],
        compiler_params=pltpu.CompilerParams(
            dimension_semantics=("arbitrary", "arbitrary"), vmem_limit_bytes=VMEM_LIMIT),
        name="mlstm",
    )(yf, yf, yb, ys, conv_w, conv_b.reshape(1, -1), bi_lane, bf_lane, m_norm_g.reshape(1, -1))


def _compress_kernel(x_ref, pos_ref, w1_ref, w2_ref, o_ref):
    nb = x_ref.shape[3]
    w1 = w1_ref[0]
    pr = jnp.dot(x_ref[0, 0, 0], w1, preferred_element_type=F32)
    pb = jnp.dot(pos_ref[0], w1, preferred_element_type=F32)
    bias = pb[0:1, :CMP_HIDDEN] + pb[8:9, CMP_HIDDEN:]
    nxt = pltpu.roll(pr[:, CMP_HIDDEN:], nb - 1, axis=0)
    hid = pr[:, :CMP_HIDDEN] + nxt + bias
    act = 0.5 * hid * (1.0 + jnp.tanh(0.7978845608028654 * (hid + 0.044715 * (hid * hid * hid))))
    o_ref[0, 0, 0] = jnp.dot(act.astype(BF16), w2_ref[0], preferred_element_type=F32).astype(BF16)


def _compress(kv, pos2, w1cat, w2, B, S):
    nb = S // CMP_STRIDE
    gw = CMP_STRIDE * N_HEAD_DIM
    kvr = kv.reshape(6, B, N_KV, nb, gw)
    return pl.pallas_call(
        _compress_kernel,
        grid=(2, B, N_KV),
        in_specs=[
            pl.BlockSpec((1, 1, 1, nb, gw), lambda s, b, g: (3 * s, b, g, 0, 0)),
            pl.BlockSpec((1, 16, gw), lambda s, b, g: (s, 0, 0)),
            pl.BlockSpec((1, gw, 2 * CMP_HIDDEN), lambda s, b, g: (s, 0, 0)),
            pl.BlockSpec((1, CMP_HIDDEN, N_HEAD_DIM), lambda s, b, g: (s, 0, 0)),
        ],
        out_specs=pl.BlockSpec((1, 1, 1, nb, N_HEAD_DIM), lambda s, b, g: (s, b, g, 0, 0)),
        out_shape=jax.ShapeDtypeStruct((2, B, N_KV, nb, N_HEAD_DIM), BF16),
        compiler_params=pltpu.CompilerParams(
            dimension_semantics=("arbitrary", "arbitrary", "arbitrary"),
            vmem_limit_bytes=VMEM_LIMIT),
        name="compress",
    )(kvr, pos2, w1cat, w2)


def _flash_update(state, s, v):
    m_old, l_old, acc = state
    m_new = jnp.maximum(m_old, jnp.max(s, axis=-1, keepdims=True))
    alpha = jnp.exp(m_old - m_new)
    p = jnp.exp(s - m_new)
    l_new = alpha * l_old + jnp.sum(p, axis=-1, keepdims=True)
    acc_new = alpha * acc + jnp.dot(p.astype(BF16), v, preferred_element_type=F32)
    return m_new, l_new, acc_new


def _nsa_kernel(q_ref, kc_ref, vc_ref, ks_ref, vs_ref, kw_ref, vw_ref, gs_ref, nz_ref, o_ref,
                kaug_ref, *, n_top):
    tq, tk, dh = NSA_TQ, NSA_TK, N_HEAD_DIM
    S = ks_ref.shape[3]
    ncmp = kc_ref.shape[3]
    g = pl.program_id(1)
    i = pl.program_id(2)
    t0 = i * tq

    @pl.when(i == 0)
    def _():
        for r in range(S // tk):
            srow = r * tk + lax.broadcasted_iota(jnp.int32, (tk, dh), 0)
            blk = lax.broadcasted_iota(jnp.int32, (tk, dh), 1)
            onehot = jnp.where(_div_sel(srow) == blk, 1.0, 0.0).astype(BF16)
            kaug_ref[r * tk:(r + 1) * tk, :] = jnp.concatenate(
                [ks_ref[0, 0, 0, r * tk:(r + 1) * tk, :], onehot], axis=1)

    qf = q_ref[...]
    qh = [qf[:, h * dh:(h + 1) * dh] for h in range(N_REP)]

    kc = kc_ref[0, 0, 0]
    vc = vc_ref[0, 0, 0]
    t_sub = t0 + lax.broadcasted_iota(jnp.int32, (tq, ncmp), 0)
    c_end = lax.broadcasted_iota(jnp.int32, (tq, ncmp), 1) * CMP_STRIDE + (CMP_BLOCK - 1)
    mask_c = c_end <= t_sub
    p_sum = jnp.zeros((tq, ncmp), F32)
    o_cmp = []
    for h in range(N_REP):
        s = jnp.where(mask_c, _nt_dot(qh[h], kc), NEG)
        e = jnp.where(mask_c, jnp.exp(s - jnp.max(s, axis=-1, keepdims=True)), 0.0)
        p = e / jnp.maximum(jnp.sum(e, axis=-1, keepdims=True), 1e-30)
        p_sum = p_sum + p
        o_cmp.append(jnp.dot(p.astype(BF16), vc, preferred_element_type=F32))

    nrow = lax.broadcasted_iota(jnp.int32, (LANES, ncmp), 0)
    ccol = lax.broadcasted_iota(jnp.int32, (LANES, ncmp), 1)
    first = _div_sel(ccol * CMP_STRIDE)
    last = _div_sel(ccol * CMP_STRIDE + CMP_BLOCK - 1)
    ov_t = (jnp.where(first == nrow, 0.5, 0.0) + jnp.where(last == nrow, 0.5, 0.0)).astype(F32)
    imp_t = lax.dot_general(ov_t, p_sum, (((1,), (1,)), ((), ())),
                            precision=lax.Precision.HIGHEST, preferred_element_type=F32)
    nblk = S // SEL_BLOCK
    imp = imp_t[0:nblk, :]
    n_io = lax.broadcasted_iota(jnp.int32, (nblk, tq), 0)
    t_lane = t0 + lax.broadcasted_iota(jnp.int32, (nblk, tq), 1)
    cur = _div_sel(t_lane)
    valid = n_io * SEL_BLOCK <= t_lane
    forced = (n_io == 0) | (n_io == cur) | (n_io == cur - 1)
    score = jnp.where(valid, imp + jnp.where(forced, FORCE, 0.0), NEG)
    cnt = jnp.zeros((nblk, tq), jnp.int32)
    for k in range(nblk):
        rk = score[k:k + 1, :]
        beats = (rk > score) | ((rk == score) & (n_io > k))
        cnt = cnt + jnp.where(beats, 1, 0)
    sel = (cnt < n_top) & valid
    bias_t = jnp.where(sel, 0.0, NEG).astype(F32)
    if nblk < LANES:
        bias_t = jnp.concatenate([bias_t, jnp.zeros((LANES - nblk, tq), F32)], axis=0)
    bias = bias_t.T[:, 0:dh].astype(BF16)
    q_aug = [jnp.concatenate([qh[h], bias], axis=1) for h in range(N_REP)]

    t_q = t0 + lax.broadcasted_iota(jnp.int32, (tq, tk), 0)
    k_off = lax.broadcasted_iota(jnp.int32, (tq, tk), 1)

    def init_state():
        return (jnp.full((tq, 1), NEG, F32), jnp.zeros((tq, 1), F32), jnp.zeros((tq, dh), F32))

    def sel_tile(kt, states, causal):
        k_t = kaug_ref[pl.ds(pl.multiple_of(kt * tk, tk), tk), :]
        v_t = vs_ref[0, 0, 0, pl.ds(pl.multiple_of(kt * tk, tk), tk), :]
        out = []
        for h in range(N_REP):
            s = _nt_dot(q_aug[h], k_t)
            if causal:
                s = jnp.where(kt * tk + k_off <= t_q, s, NEG)
            out.append(_flash_update(states[h], s, v_t))
        return tuple(out)

    st = tuple(init_state() for _ in range(N_REP))
    st = lax.fori_loop(0, i, lambda kt, c: sel_tile(kt, c, False), st)
    st = sel_tile(i, st, True)
    o_sel = [st[h][2] / st[h][1] for h in range(N_REP)]

    def win_tile(kt, states):
        k_t = kw_ref[0, 0, 0, pl.ds(pl.multiple_of(kt * tk, tk), tk), :]
        v_t = vw_ref[0, 0, 0, pl.ds(pl.multiple_of(kt * tk, tk), tk), :]
        kp = kt * tk + k_off
        band = (kp <= t_q) & (kp > t_q - WINDOW)
        out = []
        for h in range(N_REP):
            s = jnp.where(band, _nt_dot(qh[h], k_t), NEG)
            out.append(_flash_update(states[h], s, v_t))
        return tuple(out)

    sw = tuple(init_state() for _ in range(N_REP))
    sw = lax.fori_loop(jnp.maximum(i - WINDOW // tk, 0), i + 1, win_tile, sw)
    o_win = [sw[h][2] / sw[h][1] for h in range(N_REP)]

    gsig = _sigmoid(gs_ref[...])
    lane = lax.broadcasted_iota(jnp.int32, (tq, LANES), 1)
    outs = []
    for h in range(N_REP):
        base = GATE_NG0 + 3 * (g * N_REP + h)
        gate = [jnp.sum(jnp.where(lane == base + r, gsig, 0.0), axis=-1, keepdims=True)
                for r in range(3)]
        outs.append(gate[0] * o_cmp[h] + gate[1] * o_sel[h] + gate[2] * o_win[h])
    o = jnp.concatenate(outs, axis=1)
    nz = nz_ref[...]
    o_ref[...] = (o * (nz * _sigmoid(nz))).astype(BF16)


def _nsa(yb, yf, ys, cmp_kv, kv, B, S):
    tq = NSA_TQ
    nq = S // tq
    T = B * S
    nb = S // CMP_STRIDE
    n_top = min(N_SELECT, S // SEL_BLOCK)
    q_col0 = M_WIDTH // KV_W
    nz_col0 = 4 * M_WIDTH // KV_W

    def kv_spec(idx):
        return pl.BlockSpec((1, 1, 1, S, N_HEAD_DIM), lambda b, g, i: (idx, b, g, 0, 0))

    def cmp_spec(idx):
        return pl.BlockSpec((1, 1, 1, nb, N_HEAD_DIM), lambda b, g, i: (idx, b, g, 0, 0))

    return pl.pallas_call(
        functools.partial(_nsa_kernel, n_top=n_top),
        grid=(B, N_KV, nq),
        in_specs=[
            pl.BlockSpec((tq, KV_W), lambda b, g, i: (b * nq + i, q_col0 + g)),
            cmp_spec(0), cmp_spec(1),
            kv_spec(1), kv_spec(4), kv_spec(2), kv_spec(5),
            pl.BlockSpec((tq, LANES), lambda b, g, i: (b * nq + i, 0)),
            pl.BlockSpec((tq, KV_W), lambda b, g, i: (b * nq + i, nz_col0 + g)),
        ],
        out_specs=pl.BlockSpec((tq, KV_W), lambda b, g, i: (b * nq + i, g)),
        out_shape=jax.ShapeDtypeStruct((T, N_WIDTH), BF16),
        scratch_shapes=[pltpu.VMEM((S, 2 * N_HEAD_DIM), BF16)],
        compiler_params=pltpu.CompilerParams(
            dimension_semantics=("arbitrary", "arbitrary", "arbitrary"),
            vmem_limit_bytes=VMEM_LIMIT),
        name="nsa",
    )(yb, cmp_kv, cmp_kv, kv, kv, kv, kv, ys, yf)


def _out_kernel(x_ref, m_ref, n_ref, p_ref, wm_ref, wn_ref, wp_ref, wg_ref, pg_ref, fg_ref, o_ref):
    x1 = (x_ref[...]
          + jnp.dot(m_ref[...], wm_ref[...], preferred_element_type=F32)
          + jnp.dot(n_ref[...], wn_ref[...], preferred_element_type=F32))
    ple = _rms_rows(jnp.dot(p_ref[...].astype(BF16), wp_ref[...], preferred_element_type=F32),
                    pg_ref[...])
    gate = _sigmoid(jnp.dot(x1.astype(BF16), wg_ref[...], preferred_element_type=F32))
    o_ref[...] = _rms_rows(x1 + gate * ple, fg_ref[...])


def _output_stage(x2, m_out, n_out, p2, w_m, w_n, w_p, w_g, ple_norm_g, final_norm_g):
    T = x2.shape[0]
    tm = OUT_TM
    const = lambda shape: pl.BlockSpec(shape, lambda i: (0, 0), pipeline_mode=pl.Buffered(1))
    return pl.pallas_call(
        _out_kernel,
        grid=(T // tm,),
        in_specs=[
            pl.BlockSpec((tm, D_MODEL), lambda i: (i, 0)),
            pl.BlockSpec((tm, M_WIDTH), lambda i: (i, 0)),
            pl.BlockSpec((tm, N_WIDTH), lambda i: (i, 0)),
            pl.BlockSpec((tm, PLE_DIM), lambda i: (i, 0)),
            const((M_WIDTH, D_MODEL)),
            const((N_WIDTH, D_MODEL)),
            const((PLE_DIM, D_MODEL)),
            const((D_MODEL, D_MODEL)),
            const((1, D_MODEL)),
            const((1, D_MODEL)),
        ],
        out_specs=pl.BlockSpec((tm, D_MODEL), lambda i: (i, 0)),
        out_shape=jax.ShapeDtypeStruct((T, D_MODEL), F32),
        compiler_params=pltpu.CompilerParams(
            dimension_semantics=("arbitrary",), vmem_limit_bytes=VMEM_LIMIT),
        name="out_stage",
    )(x2, m_out, n_out, p2, w_m, w_n, w_p, w_g,
      ple_norm_g.reshape(1, -1), final_norm_g.reshape(1, -1))


def _split_w_in(w):
    sizes = (M_WIDTH,) * 5 + (M_HEADS, M_HEADS) + (N_WIDTH,) + (KV_W,) * 6 + (3 * N_HEADS, N_WIDTH)
    cuts = np.cumsum(sizes)[:-1].tolist()
    return jnp.split(w, cuts, axis=1)


def _rope_lane_consts():
    half = ROT_DIM // 2
    inv = ROPE_THETA ** (-(jnp.arange(half, dtype=F32) * 2.0 / ROT_DIM))
    d = np.arange(LANES) % N_HEAD_DIM
    inv_lane = jnp.where(d < ROT_DIM, inv[d % half], 0.0).astype(F32).reshape(1, LANES)
    sa = jnp.asarray(np.where(d < half, -1.0, 0.0), F32).reshape(1, LANES)
    sb = jnp.asarray(np.where((d >= half) & (d < ROT_DIM), 1.0, 0.0), F32).reshape(1, LANES)
    return inv_lane, sa, sb


def kernel(x, p, positions, norm_g, w_in, conv_w, conv_b, b_igate, b_fgate, m_norm_g, cmp_pos_k, cmp_pos_v, cmp_k_w1, cmp_k_w2, cmp_v_w1, cmp_v_w2, w_out, ple_gate_w, ple_proj_w, ple_norm_g, final_norm_g):
    B, S, _ = x.shape
    T = B * S
    assert w_in.shape[0] == 1, "single-layer block"
    assert S % max(MLSTM_L, NSA_TQ, PROJ_TM) == 0 and S // SEL_BLOCK <= N_HEAD_DIM
    l = 0

    (w_mq, w_mk, w_mv, w_mo, w_mz, w_mi, w_mf, w_nq, w_kc, w_vc, w_ks, w_vs, w_kw, w_vw,
     w_ng, w_nz) = _split_w_in(w_in[l])
    w_f32 = jnp.concatenate([w_mq, w_mk, w_mo, w_mz, w_nz], axis=1).astype(BF16)
    pad = jnp.zeros((D_MODEL, LANES - 2 * M_HEADS - 3 * N_HEADS), F32)
    w_small = jnp.concatenate([w_mi, w_mf, w_ng, pad], axis=1).astype(BF16)
    w_bf16 = jnp.concatenate([w_mv, w_nq * (N_HEAD_DIM ** -0.5), w_kc, w_ks, w_kw, w_vc, w_vs, w_vw],
                             axis=1).astype(BF16)

    x2 = x.reshape(T, D_MODEL)
    pos_col = positions.astype(F32).reshape(T, 1)
    yf, ys, yb = _input_projections(x2, norm_g[l], pos_col, w_f32, w_small, w_bf16,
                                    _rope_lane_consts())

    lane_pad = jnp.zeros((LANES - 2 * M_HEADS,), F32)
    bi_lane = jnp.concatenate([b_igate[l], jnp.zeros((M_HEADS,), F32), lane_pad]).reshape(1, LANES)
    bf_lane = jnp.concatenate([jnp.zeros((M_HEADS,), F32), b_fgate[l], lane_pad]).reshape(1, LANES)
    m_out = _mlstm(yf, yb, ys, conv_w[l], conv_b[l], bi_lane, bf_lane, m_norm_g[l], B, S)

    kv = yb[:, M_WIDTH + N_WIDTH:].reshape(B, S, 6, N_KV, N_HEAD_DIM).transpose(2, 0, 3, 1, 4)
    gw = CMP_STRIDE * N_HEAD_DIM

    def w1cat(w1):
        return jnp.concatenate([w1[:gw], w1[gw:]], axis=1)

    def pos_rows(pos):
        flat = pos.reshape(2, gw)
        return jnp.concatenate([jnp.broadcast_to(flat[0:1], (8, gw)),
                                jnp.broadcast_to(flat[1:2], (8, gw))], axis=0)

    pos2 = jnp.stack([pos_rows(cmp_pos_k[l]), pos_rows(cmp_pos_v[l])]).astype(BF16)
    w1 = jnp.stack([w1cat(cmp_k_w1[l]), w1cat(cmp_v_w1[l])]).astype(BF16)
    w2 = jnp.stack([cmp_k_w2[l], cmp_v_w2[l]]).astype(BF16)
    cmp_kv = _compress(kv, pos2, w1, w2, B, S)

    n_out = _nsa(yb, yf, ys, cmp_kv, kv, B, S)

    w_o = w_out[l].astype(BF16)
    out = _output_stage(x2, m_out, n_out, p[l].reshape(T, PLE_DIM),
                        w_o[:M_WIDTH], w_o[M_WIDTH:], ple_proj_w[l].astype(BF16),
                        ple_gate_w[l].astype(BF16), ple_norm_g[l], final_norm_g)
    return out.reshape(B, S, D_MODEL)
```
